```python
import jax
import jax.numpy as jnp
from jax import lax
import numpy as np

D_MODEL = 1024
BATCH = 32
SEQ = 2048
DEPTH = 4
DEC_BATCH = 8
DEC_SEQ = 64
PAST_LEN = 2048

CHUNK = 64
QBLOCK = 128
D_FF = 2816
FFN_RES_WEIGHT = 0.5
MLA_HEADS = 8
Q_LORA = 384
KV_LORA = 256
QK_NOPE = 64
QK_ROPE = 32
V_DIM = 64
MLA_WIDTH = MLA_HEADS * V_DIM
MLA_SCALE = (QK_NOPE + QK_ROPE) ** -0.5
ROPE_BASE = 10000.0
HG_HEADS = 4
HG_DK = 128
HG_DV = 128
HG_KWIDTH = HG_HEADS * HG_DK
HG_WIDTH = HG_HEADS * HG_DV
MIX_WIDTH = MLA_WIDTH + HG_WIDTH
IN_SPLITS = (Q_LORA, Q_LORA + KV_LORA, Q_LORA + KV_LORA + QK_ROPE,
             Q_LORA + KV_LORA + QK_ROPE + HG_KWIDTH,
             Q_LORA + KV_LORA + QK_ROPE + 2 * HG_KWIDTH,
             Q_LORA + KV_LORA + QK_ROPE + 2 * HG_KWIDTH + HG_WIDTH)
IN_COLS = Q_LORA + KV_LORA + QK_ROPE + 2 * HG_KWIDTH + 2 * HG_WIDTH
NORM_EPS = 1e-6
NEG_INF = -1e30

kernel_name = 'hymba_mla_hgrn2_macaron_stream'


def rmsnorm(x, g):
    xf = x.astype(jnp.float32)
    out = xf * lax.rsqrt(jnp.mean(xf * xf, axis=-1, keepdims=True) + NORM_EPS) * g.astype(jnp.float32)
    return out.astype(x.dtype)


def rope(x, pos):
    half = x.shape[-1] // 2
    inv = ROPE_BASE ** (-jnp.arange(half, dtype=jnp.float32) / half)
    ang = pos.astype(jnp.float32)[:, None] * inv
    ang = ang.reshape((x.shape[1],) + (1,) * (x.ndim - 3) + (half,))
    cos = jnp.cos(ang).astype(x.dtype)
    sin = jnp.sin(ang).astype(x.dtype)
    x1, x2 = x[..., :half], x[..., half:]
    return jnp.concatenate([x1 * cos - x2 * sin, x1 * sin + x2 * cos], axis=-1)


def swiglu(h, w_gu, w_down):
    gate, up = jnp.split(h @ w_gu, 2, axis=-1)
    return (jax.nn.silu(gate) * up) @ w_down


def mla_attention(q_nope, q_rope, lat, krope, w_ukv, offset):
    B, T, H, _ = q_nope.shape
    L = lat.shape[1]
    kv = (lat @ w_ukv).reshape(B, L, H, QK_NOPE + V_DIM)
    k_nope, v = kv[..., :QK_NOPE], kv[..., QK_NOPE:]
    outs = []
    for start in range(0, T, QBLOCK):
        stop = min(start + QBLOCK, T)
        kend = min(L, offset + stop)
        s = (jnp.einsum('bqhd,bkhd->bhqk', q_nope[:, start:stop], k_nope[:, :kend])
             + jnp.einsum('bqhr,bkr->bhqk', q_rope[:, start:stop], krope[:, :kend]))
        s = s.astype(jnp.float32) * MLA_SCALE
        q_chunk = (offset + jnp.arange(start, stop)) // CHUNK
        k_chunk = jnp.arange(kend) // CHUNK
        mask = k_chunk[None, :] <= q_chunk[:, None]
        p = jax.nn.softmax(jnp.where(mask, s, NEG_INF), axis=-1).astype(v.dtype)
        outs.append(jnp.einsum('bhqk,bkhd->bqhd', p, v[:, :kend]))
    return jnp.concatenate(outs, axis=1).reshape(B, T, H * V_DIM)


def hgrn2_chunked(q, k, log_f, v, s0):
    B, T, H, DK = q.shape
    DV = v.shape[-1]
    pad = (-T) % CHUNK
    if pad:
        padw = ((0, 0), (0, pad), (0, 0), (0, 0))
        q, k, log_f, v = [jnp.pad(a, padw) for a in (q, k, log_f, v)]
    NC = (T + pad) // CHUNK

    def to_chunks(a):
        return jnp.moveaxis(a.reshape(B, NC, CHUNK, H, a.shape[-1]), 2, 0)

    qc, kc, fc, vc = to_chunks(q), to_chunks(k), to_chunks(log_f), to_chunks(v)

    def intra_step(s, inp):
        q_t, k_t, lf_t, v_t = inp
        s = s * jnp.exp(lf_t)[..., None] + k_t[..., None] * v_t[..., None, :]
        return s, jnp.einsum('bnhk,bnhkv->bnhv', q_t, s)

    s_loc, o_loc = lax.scan(intra_step, jnp.zeros((B, NC, H, DK, DV), q.dtype), (qc, kc, fc, vc))
    a = jnp.cumsum(fc, axis=0)

    def inter_step(s, inp):
        dec, sl = inp
        return s * dec[..., None] + sl, s

    s_final, s_in = lax.scan(inter_step, s0,
                             (jnp.moveaxis(jnp.exp(a[-1]), 1, 0), jnp.moveaxis(s_loc, 1, 0)))
    o_inter = jnp.einsum('cbnhk,nbhkv->cbnhv', qc * jnp.exp(a), s_in)
    o = jnp.moveaxis(o_loc + o_inter, 0, 2).reshape(B, NC * CHUNK, H, DV)[:, :T]
    return o, s_final


def token_mixer(h, pos, lat_past, krope_past, s0, lb, p):
    B, T, _ = h.shape
    cq, ckv, kr, hq, hf, hi, hg = jnp.split(h @ p['w_in'], IN_SPLITS, axis=-1)
    q = (rmsnorm(cq, p['q_norm_gain']) @ p['w_uq']).reshape(B, T, MLA_HEADS, QK_NOPE + QK_ROPE)
    q_nope, q_rope = q[..., :QK_NOPE], rope(q[..., QK_NOPE:], pos)
    ckv = rmsnorm(ckv, p['kv_norm_gain'])
    krope = rope(kr, pos)
    if lat_past is None:
        keys_lat, keys_rope, offset = ckv, krope, 0
    else:
        keys_lat = jnp.concatenate([lat_past, ckv], axis=1)
        keys_rope = jnp.concatenate([krope_past, krope], axis=1)
        offset = lat_past.shape[1]
    attn = rmsnorm(mla_attention(q_nope, q_rope, keys_lat, keys_rope, p['w_ukv'], offset), p['mla_out_gain'])
    z = hf.astype(jnp.float32)
    log_f = jnp.logaddexp(jnp.log(lb), jnp.log1p(-lb) + jax.nn.log_sigmoid(z))
    k = (1.0 - lb) * jax.nn.sigmoid(-z)
    o, s_final = hgrn2_chunked(hq.astype(jnp.float32).reshape(B, T, HG_HEADS, HG_DK),
                               k.reshape(B, T, HG_HEADS, HG_DK),
                               log_f.reshape(B, T, HG_HEADS, HG_DK),
                               hi.astype(jnp.float32).reshape(B, T, HG_HEADS, HG_DV),
                               s0.astype(jnp.float32))
    o = rmsnorm(o, p['hg_norm_gain']) * jax.nn.silu(hg.astype(jnp.float32).reshape(B, T, HG_HEADS, HG_DV))
    o = o.reshape(B, T, HG_WIDTH).astype(h.dtype)
    y = jnp.concatenate([attn, o], axis=-1) @ p['w_out']
    return y, ckv, krope, s_final.astype(h.dtype)


def layer_forward(x, pos, lat_past, krope_past, s0, lb, p):
    g = p['norm_gains']
    x = x + FFN_RES_WEIGHT * rmsnorm(swiglu(rmsnorm(x, g[0]), p['ffa_w_gu'], p['ffa_w_down']), g[1])
    y, ckv, krope, s_final = token_mixer(rmsnorm(x, g[2]), pos, lat_past, krope_past, s0, lb, p)
    x = x + rmsnorm(y, g[3])
    x = x + FFN_RES_WEIGHT * rmsnorm(swiglu(rmsnorm(x, g[4]), p['ffb_w_gu'], p['ffb_w_down']), g[5])
    return x, ckv, krope, s_final


def setup_inputs(seed: int = 0) -> dict:
    key = jax.random.key(seed)
    ks = jax.random.split(key, 19)

    def nrm(k, shape, scale):
        return jax.random.normal(k, shape, jnp.float32) * scale

    def gain(k, shape):
        return 1.0 + 0.05 * jax.random.normal(k, shape, jnp.float32)

    return {
        'x_prompt': nrm(ks[0], (BATCH, SEQ, D_MODEL), 1.0),
        'x_sample': nrm(ks[1], (DEC_BATCH, DEC_SEQ, D_MODEL), 1.0),
        'cache_mla_latent': nrm(ks[2], (DEPTH, DEC_BATCH, PAST_LEN, KV_LORA), 1.0),
        'cache_mla_krope': nrm(ks[3], (DEPTH, DEC_BATCH, PAST_LEN, QK_ROPE), 1.0),
        'state_hgrn': nrm(ks[4], (DEPTH, DEC_BATCH, HG_HEADS, HG_DK, HG_DV), 0.3),
        'norm_gains': gain(ks[5], (DEPTH, 6, D_MODEL)),
        'ffa_w_gu': nrm(ks[6], (DEPTH, D_MODEL, 2 * D_FF), D_MODEL ** -0.5),
        'ffa_w_down': nrm(ks[7], (DEPTH, D_FF, D_MODEL), D_FF ** -0.5),
        'ffb_w_gu': nrm(ks[8], (DEPTH, D_MODEL, 2 * D_FF), D_MODEL ** -0.5),
        'ffb_w_down': nrm(ks[9], (DEPTH, D_FF, D_MODEL), D_FF ** -0.5),
        'w_in': nrm(ks[10], (DEPTH, D_MODEL, IN_COLS), D_MODEL ** -0.5),
        'q_norm_gain': gain(ks[11], (DEPTH, Q_LORA)),
        'w_uq': nrm(ks[12], (DEPTH, Q_LORA, MLA_HEADS * (QK_NOPE + QK_ROPE)), Q_LORA ** -0.5),
        'kv_norm_gain': gain(ks[13], (DEPTH, KV_LORA)),
        'w_ukv': nrm(ks[14], (DEPTH, KV_LORA, MLA_HEADS * (QK_NOPE + V_DIM)), KV_LORA ** -0.5),
        'mla_out_gain': gain(ks[15], (DEPTH, MLA_WIDTH)),
        'hg_lb_logits': nrm(ks[16], (DEPTH, HG_KWIDTH), 0.5),
        'hg_norm_gain': gain(ks[17], (DEPTH, HG_DV)),
        'w_out': nrm(ks[18], (DEPTH, MIX_WIDTH, D_MODEL), MIX_WIDTH ** -0.5),
    }


def reference(x_prompt, x_sample, cache_mla_latent, cache_mla_krope, state_hgrn,
              norm_gains, ffa_w_gu, ffa_w_down, ffb_w_gu, ffb_w_down, w_in,
              q_norm_gain, w_uq, kv_norm_gain, w_ukv, mla_out_gain,
              hg_lb_logits, hg_norm_gain, w_out):
    lbs = jnp.cumsum(jax.nn.softmax(hg_lb_logits.astype(jnp.float32), axis=0), axis=0)
    lbs = lbs - lbs[0:1]
    past_len = cache_mla_latent.shape[2]
    pos_p = jnp.arange(x_prompt.shape[1])
    pos_s = past_len + jnp.arange(x_sample.shape[1])
    s0_p = jnp.zeros((x_prompt.shape[0], HG_HEADS, HG_DK, HG_DV), x_prompt.dtype)
    h_p, h_s = x_prompt, x_sample
    lat_p, kr_p, st_p, lat_s, kr_s, st_s = [], [], [], [], [], []
    for l in range(DEPTH):
        p = {'norm_gains': norm_gains[l], 'ffa_w_gu': ffa_w_gu[l], 'ffa_w_down': ffa_w_down[l],
             'ffb_w_gu': ffb_w_gu[l], 'ffb_w_down': ffb_w_down[l], 'w_in': w_in[l],
             'q_norm_gain': q_norm_gain[l], 'w_uq': w_uq[l], 'kv_norm_gain': kv_norm_gain[l],
             'w_ukv': w_ukv[l], 'mla_out_gain': mla_out_gain[l], 'hg_norm_gain': hg_norm_gain[l],
             'w_out': w_out[l]}
        h_p, c, k, s = layer_forward(h_p, pos_p, None, None, s0_p, lbs[l], p)
        lat_p.append(c)
        kr_p.append(k)
        st_p.append(s)
        h_s, c, k, s = layer_forward(h_s, pos_s, cache_mla_latent[l], cache_mla_krope[l],
                                     state_hgrn[l], lbs[l], p)
        lat_s.append(c)
        kr_s.append(k)
        st_s.append(s)
    return (h_p, h_s, jnp.stack(lat_p), jnp.stack(kr_p), jnp.stack(st_p),
            jnp.stack(lat_s), jnp.stack(kr_s), jnp.stack(st_s))
```

```python
import functools

import jax
import jax.numpy as jnp
from jax import lax
from jax.experimental import pallas as pl
from jax.experimental.pallas import tpu as pltpu

D_MODEL = 1024
CHUNK = 64
D_FF = 2816
FFN_RES_WEIGHT = 0.5
MLA_HEADS = 8
Q_LORA = 384
KV_LORA = 256
QK_NOPE = 64
QK_ROPE = 32
V_DIM = 64
MLA_WIDTH = MLA_HEADS * V_DIM
MLA_SCALE = (QK_NOPE + QK_ROPE) ** -0.5
ROPE_BASE = 10000.0
HG_HEADS = 4
HG_DK = 128
HG_DV = 128
HG_KWIDTH = HG_HEADS * HG_DK
HG_WIDTH = HG_HEADS * HG_DV
NORM_EPS = 1e-6
NEG_INF = -1e30

LANES = 128
HEAD_PAD = LANES
MLA_PAD = MLA_HEADS * HEAD_PAD
VMEM_LIMIT_BYTES = 56 * 1024 * 1024

FF_CHUNK = 256
TOKEN_TILE = 512
SEQ_TILE = 256
IN_COLS_PAD = Q_LORA + KV_LORA + 2 * HG_KWIDTH + 2 * HG_WIDTH + LANES
HG_FAST_EXP_LIMIT = 75.0

_BF16 = jnp.bfloat16
_F32 = jnp.float32


def _rms(x, g):
    ms = jnp.mean(x * x, axis=-1, keepdims=True)
    return x * lax.rsqrt(ms + NORM_EPS) * g


def _dot(a, b):
    return jnp.dot(a, b, preferred_element_type=_F32)


def _dot_nt(a, b):
    return lax.dot_general(a, b, (((1,), (1,)), ((), ())), preferred_element_type=_F32)


def _dot_tn(a, b):
    return lax.dot_general(a, b, (((0,), (0,)), ((), ())), preferred_element_type=_F32)


def _const_spec(shape):
    nd = len(shape)
    return pl.BlockSpec(shape, lambda *_: (0,) * nd, pipeline_mode=pl.Buffered(1))


def _ffn_half_step(x, g_in, g_out, wgu_ref, wdown_ref):
    h = _rms(x, g_in).astype(_BF16)
    acc = jnp.zeros(x.shape, _F32)
    for c in range(D_FF // FF_CHUNK):
        lo = c * FF_CHUNK
        gate = _dot(h, wgu_ref[:, lo:lo + FF_CHUNK])
        up = _dot(h, wgu_ref[:, D_FF + lo:D_FF + lo + FF_CHUNK])
        act = (gate * jax.nn.sigmoid(gate) * up).astype(_BF16)
        acc = acc + _dot(act, wdown_ref[lo:lo + FF_CHUNK, :])
    return x + FFN_RES_WEIGHT * _rms(acc, g_out)


def _ffn_kernel(x_ref, g_ref, wgu_ref, wdown_ref, o_ref, *, gain_rows):
    gi, go = gain_rows
    o_ref[...] = _ffn_half_step(x_ref[...], g_ref[gi:gi + 1, :], g_ref[go:go + 1, :],
                                wgu_ref, wdown_ref)


def _ffn_call(x2d, gains, wgu, wdown, gain_rows):
    n = x2d.shape[0]
    tm = TOKEN_TILE if n % TOKEN_TILE == 0 else n
    return pl.pallas_call(
        functools.partial(_ffn_kernel, gain_rows=gain_rows),
        grid=(n // tm,),
        in_specs=[
            pl.BlockSpec((tm, D_MODEL), lambda i: (i, 0)),
            _const_spec(gains.shape),
            _const_spec(wgu.shape),
            _const_spec(wdown.shape),
        ],
        out_specs=pl.BlockSpec((tm, D_MODEL), lambda i: (i, 0)),
        out_shape=jax.ShapeDtypeStruct(x2d.shape, _F32),
        compiler_params=pltpu.CompilerParams(
            dimension_semantics=("arbitrary",), vmem_limit_bytes=VMEM_LIMIT_BYTES),
        name="ffn",
    )(x2d, gains, wgu, wdown)


def _out_kernel(x_ref, attn_ref, hg_ref, g_ref, wout_ref, wgu_ref, wdown_ref, o_ref):
    y = _dot(attn_ref[...], wout_ref[0:MLA_WIDTH, :]) + _dot(hg_ref[...], wout_ref[MLA_WIDTH:, :])
    x = x_ref[...] + _rms(y, g_ref[3:4, :])
    o_ref[...] = _ffn_half_step(x, g_ref[4:5, :], g_ref[5:6, :], wgu_ref, wdown_ref)


def _out_call(x2d, attn2d, hg2d, gains, wout, wgu, wdown):
    n = x2d.shape[0]
    tm = TOKEN_TILE if n % TOKEN_TILE == 0 else n
    return pl.pallas_call(
        _out_kernel,
        grid=(n // tm,),
        in_specs=[
            pl.BlockSpec((tm, D_MODEL), lambda i: (i, 0)),
            pl.BlockSpec((tm, MLA_WIDTH), lambda i: (i, 0)),
            pl.BlockSpec((tm, HG_WIDTH), lambda i: (i, 0)),
            _const_spec(gains.shape),
            _const_spec(wout.shape),
            _const_spec(wgu.shape),
            _const_spec(wdown.shape),
        ],
        out_specs=pl.BlockSpec((tm, D_MODEL), lambda i: (i, 0)),
        out_shape=jax.ShapeDtypeStruct(x2d.shape, _F32),
        compiler_params=pltpu.CompilerParams(
            dimension_semantics=("arbitrary",), vmem_limit_bytes=VMEM_LIMIT_BYTES),
        name="out_ffn",
    )(x2d, attn2d, hg2d, gains, wout, wgu, wdown)


def _chunk_cumsum(x, row_in_chunk):
    shift = 1
    while shift < CHUNK:
        moved = pltpu.roll(x, shift, 0)
        x = x + jnp.where(row_in_chunk >= shift, moved, 0.0)
        shift *= 2
    return x


def _hgrn_lower_bound(logits, layer):
    m = jnp.max(logits, axis=0, keepdims=True)
    e = jnp.exp(logits - m)
    den = jnp.sum(e, axis=0, keepdims=True)
    num = jnp.zeros_like(den)
    for j in range(1, layer + 1):
        num = num + e[j:j + 1, :]
    return num / den


def _proj_kernel(*refs, layer, tq, has_s0):
    if has_s0:
        (x_ref, g_ref, win_ref, qg_ref, wuq_ref, kvg_ref, wkl_ref, wkr_ref, wv_ref,
         ropeq_ref, ropek_ref, lbl_ref, hgg_ref, s0_ref,
         q_ref, k_ref, v_ref, lat_ref, kr_ref, ohg_ref, st_out_ref,
         st_ref, qs_ref, ks_ref, vs_ref, as_ref, oi_ref) = refs
    else:
        (x_ref, g_ref, win_ref, qg_ref, wuq_ref, kvg_ref, wkl_ref, wkr_ref, wv_ref,
         ropeq_ref, ropek_ref, lbl_ref, hgg_ref,
         q_ref, k_ref, v_ref, lat_ref, kr_ref, ohg_ref, st_out_ref,
         st_ref, qs_ref, ks_ref, vs_ref, as_ref, oi_ref) = refs
        s0_ref = None
    t = pl.program_id(1)
    n_t = pl.num_programs(1)
    n_chunks = tq // CHUNK

    @pl.when(t == 0)
    def _():
        for hh in range(HG_HEADS):
            if has_s0:
                st_ref[hh] = jnp.transpose(s0_ref[0, hh])
            else:
                st_ref[hh] = jnp.zeros((HG_DV, HG_DK), _F32)

    h = _rms(x_ref[0], g_ref[2:3, :]).astype(_BF16)
    proj = _dot(h, win_ref[...])
    o_q = 0
    o_kv = Q_LORA
    o_hq = o_kv + KV_LORA
    o_hf = o_hq + HG_KWIDTH
    o_hi = o_hf + HG_KWIDTH
    o_hgate = o_hi + HG_WIDTH
    o_kr = o_hgate + HG_WIDTH

    cqn = _rms(proj[:, o_q:o_q + Q_LORA], qg_ref[...]).astype(_BF16)
    qfull = _dot(cqn, wuq_ref[...]) * MLA_SCALE
    cq_t, s1q, s2q = ropeq_ref[0], ropeq_ref[1], ropeq_ref[2]
    for hh in range(MLA_HEADS):
        qh = qfull[:, hh * HEAD_PAD:(hh + 1) * HEAD_PAD]
        qh = (qh * cq_t + pltpu.roll(qh, HEAD_PAD - QK_ROPE // 2, 1) * s1q
              + pltpu.roll(qh, QK_ROPE // 2, 1) * s2q)
        q_ref[0, :, hh * HEAD_PAD:(hh + 1) * HEAD_PAD] = qh.astype(_BF16)

    ckvn = _rms(proj[:, o_kv:o_kv + KV_LORA], kvg_ref[...])
    lat_ref[0] = ckvn
    kr = proj[:, o_kr:o_kr + LANES]
    kr = (kr * ropek_ref[0] + pltpu.roll(kr, LANES - QK_ROPE // 2, 1) * ropek_ref[1]
          + pltpu.roll(kr, QK_ROPE // 2, 1) * ropek_ref[2])
    kr_ref[0] = kr[:, 0:QK_ROPE]
    ckvn_b = ckvn.astype(_BF16)
    k_ref[0] = (_dot(ckvn_b, wkl_ref[...]) + _dot(kr.astype(_BF16), wkr_ref[...])).astype(_BF16)
    v_ref[0] = _dot(ckvn_b, wv_ref[...]).astype(_BF16)

    lb_all = _hgrn_lower_bound(lbl_ref[...], layer)
    row_in_chunk = lax.broadcasted_iota(jnp.int32, (tq, HG_DK), 0) & (CHUNK - 1)
    tri = (lax.broadcasted_iota(jnp.int32, (CHUNK, CHUNK), 0)
           >= lax.broadcasted_iota(jnp.int32, (CHUNK, CHUNK), 1))
    row_c = lax.broadcasted_iota(jnp.int32, (CHUNK, 1), 0)
    for hh in range(HG_HEADS):
        sl = slice(hh * HG_DK, (hh + 1) * HG_DK)
        lb = lb_all[:, sl]
        log_lb = jnp.log(lb)
        log_1mlb = jnp.log1p(-lb)
        z = proj[:, o_hf + hh * HG_DK:o_hf + (hh + 1) * HG_DK]
        q = proj[:, o_hq + hh * HG_DK:o_hq + (hh + 1) * HG_DK]
        v = proj[:, o_hi + hh * HG_DV:o_hi + (hh + 1) * HG_DV]
        gate = proj[:, o_hgate + hh * HG_DV:o_hgate + (hh + 1) * HG_DV]
        ez = jnp.exp(-jnp.abs(z))
        rz = 1.0 / (1.0 + ez)
        log_sig = jnp.minimum(z, 0.0) - jnp.log1p(ez)
        b = log_1mlb + log_sig
        log_f = jnp.maximum(log_lb, b) + jnp.log1p(jnp.exp(-jnp.abs(log_lb - b)))
        kk = (1.0 - lb) * jnp.where(z >= 0.0, ez * rz, rz)
        a = _chunk_cumsum(log_f, row_in_chunk)

        worst = jnp.zeros((1, HG_DK), _F32)
        for c in range(n_chunks):
            r0 = c * CHUNK
            a_mid = a[r0 + CHUNK // 2 - 1:r0 + CHUNK // 2, :]
            worst = jnp.maximum(worst, jnp.maximum(a[r0:r0 + 1, :] - a_mid,
                                                   a_mid - a[r0 + CHUNK - 1:r0 + CHUNK, :]))
        use_exact = jnp.max(worst) > HG_FAST_EXP_LIMIT

        @pl.when(jnp.logical_not(use_exact))
        def _():
            for c in range(n_chunks):
                r0 = c * CHUNK
                a_c = a[r0:r0 + CHUNK, :]
                a_mid = a_c[CHUNK // 2 - 1:CHUNK // 2, :]
                qg = (q[r0:r0 + CHUNK, :] * jnp.exp(a_c - a_mid)).astype(_BF16)
                kg = (kk[r0:r0 + CHUNK, :] * jnp.exp(a_mid - a_c)).astype(_BF16)
                amat = jnp.where(tri, _dot_nt(qg, kg), 0.0).astype(_BF16)
                oi_ref[r0:r0 + CHUNK, :] = _dot(amat, v[r0:r0 + CHUNK, :].astype(_BF16))

        @pl.when(use_exact)
        def _():
            qs_ref[...] = q
            ks_ref[...] = kk
            vs_ref[...] = v
            as_ref[...] = a
            for c in range(n_chunks):
                r0 = c * CHUNK
                q_c = qs_ref[r0:r0 + CHUNK, :]
                a_c = as_ref[r0:r0 + CHUNK, :]

                def body(s, o, r0=r0, q_c=q_c, a_c=a_c):
                    a_s = as_ref[pl.ds(r0 + s, 1), :]
                    k_s = ks_ref[pl.ds(r0 + s, 1), :]
                    v_s = vs_ref[pl.ds(r0 + s, 1), :]
                    w = q_c * jnp.exp(jnp.minimum(a_c - a_s, 0.0)) * k_s
                    col = jnp.sum(w, axis=1, keepdims=True)
                    col = jnp.where(row_c >= s, col, 0.0)
                    return o + col * v_s

                oi_ref[r0:r0 + CHUNK, :] = lax.fori_loop(
                    0, CHUNK, body, jnp.zeros((CHUNK, HG_DV), _F32))

        o_intra = oi_ref[...]
        st = st_ref[hh]
        o_parts = []
        for c in range(n_chunks):
            r0 = c * CHUNK
            a_c = a[r0:r0 + CHUNK, :]
            a_end = a_c[CHUNK - 1:CHUNK, :]
            qa = (q[r0:r0 + CHUNK, :] * jnp.exp(a_c)).astype(_BF16)
            o_parts.append(_dot_nt(qa, st.astype(_BF16)))
            kdec = (kk[r0:r0 + CHUNK, :] * jnp.exp(a_end - a_c)).astype(_BF16)
            st = st * jnp.exp(a_end) + _dot_tn(v[r0:r0 + CHUNK, :].astype(_BF16), kdec)
        st_ref[hh] = st
        o = o_intra + jnp.concatenate(o_parts, axis=0)
        o = _rms(o, hgg_ref[...]) * (gate * jax.nn.sigmoid(gate))
        ohg_ref[0, :, hh * HG_DV:(hh + 1) * HG_DV] = o.astype(_BF16)

    @pl.when(t == n_t - 1)
    def _():
        for hh in range(HG_HEADS):
            st_out_ref[0, hh] = jnp.transpose(st_ref[hh])


def _proj_call(x, gains, w_in, q_gain, w_uq, kv_gain, w_kl, w_kr, w_v, rope_q, rope_k,
               lb_logits, hg_gain, s0, layer):
    bsz, seq, _ = x.shape
    tq = SEQ_TILE if seq % SEQ_TILE == 0 else seq
    assert seq % tq == 0 and tq % CHUNK == 0
    has_s0 = s0 is not None
    seq_block = lambda w: pl.BlockSpec((1, tq, w), lambda b, t: (b, t, 0))
    in_specs = [
        seq_block(D_MODEL),
        _const_spec(gains.shape),
        _const_spec(w_in.shape),
        _const_spec(q_gain.shape),
        _const_spec(w_uq.shape),
        _const_spec(kv_gain.shape),
        _const_spec(w_kl.shape),
        _const_spec(w_kr.shape),
        _const_spec(w_v.shape),
        pl.BlockSpec((3, tq, LANES), lambda b, t: (0, t, 0)),
        pl.BlockSpec((3, tq, LANES), lambda b, t: (0, t, 0)),
        _const_spec(lb_logits.shape),
        _const_spec(hg_gain.shape),
    ]
    args = [x, gains, w_in, q_gain, w_uq, kv_gain, w_kl, w_kr, w_v, rope_q, rope_k,
            lb_logits, hg_gain]
    state_spec = pl.BlockSpec((1, HG_HEADS, HG_DK, HG_DV), lambda b, t: (b, 0, 0, 0))
    if has_s0:
        in_specs.append(state_spec)
        args.append(s0)
    out_shape = (
        jax.ShapeDtypeStruct((bsz, seq, MLA_PAD), _BF16),
        jax.ShapeDtypeStruct((bsz, seq, MLA_PAD), _BF16),
        jax.ShapeDtypeStruct((bsz, seq, MLA_PAD), _BF16),
        jax.ShapeDtypeStruct((bsz, seq, KV_LORA), _F32),
        jax.ShapeDtypeStruct((bsz, seq, QK_ROPE), _F32),
        jax.ShapeDtypeStruct((bsz, seq, HG_WIDTH), _BF16),
        jax.ShapeDtypeStruct((bsz, HG_HEADS, HG_DK, HG_DV), _F32),
    )
    out_specs = (
        seq_block(MLA_PAD), seq_block(MLA_PAD), seq_block(MLA_PAD),
        seq_block(KV_LORA), seq_block(QK_ROPE), seq_block(HG_WIDTH), state_spec,
    )
    scratch = [pltpu.VMEM((HG_HEADS, HG_DV, HG_DK), _F32)] + [
        pltpu.VMEM((tq, HG_DK), _F32) for _ in range(5)]
    return pl.pallas_call(
        functools.partial(_proj_kernel, layer=layer, tq=tq, has_s0=has_s0),
        grid=(bsz, seq // tq),
        in_specs=in_specs,
        out_specs=out_specs,
        out_shape=out_shape,
        scratch_shapes=scratch,
        compiler_params=pltpu.CompilerParams(
            dimension_semantics=("arbitrary", "arbitrary"), vmem_limit_bytes=VMEM_LIMIT_BYTES),
        name="proj_hgrn",
    )(*args)


def _past_kv_kernel(lat_ref, kr_ref, wkl_ref, wkr_ref, wv_ref, k_ref, v_ref):
    lat = lat_ref[...].astype(_BF16)
    k_ref[...] = (_dot(lat, wkl_ref[...])
                  + _dot(kr_ref[...].astype(_BF16), wkr_ref[0:QK_ROPE, :])).astype(_BF16)
    v_ref[...] = _dot(lat, wv_ref[...]).astype(_BF16)


def _past_kv_call(lat2d, kr2d, w_kl, w_kr, w_v):
    n = lat2d.shape[0]
    tm = TOKEN_TILE if n % TOKEN_TILE == 0 else n
    return pl.pallas_call(
        _past_kv_kernel,
        grid=(n // tm,),
        in_specs=[
            pl.BlockSpec((tm, KV_LORA), lambda i: (i, 0)),
            pl.BlockSpec((tm, QK_ROPE), lambda i: (i, 0)),
            _const_spec(w_kl.shape),
            _const_spec(w_kr.shape),
            _const_spec(w_v.shape),
        ],
        out_specs=(pl.BlockSpec((tm, MLA_PAD), lambda i: (i, 0)),
                   pl.BlockSpec((tm, MLA_PAD), lambda i: (i, 0))),
        out_shape=(jax.ShapeDtypeStruct((n, MLA_PAD), _BF16),
                   jax.ShapeDtypeStruct((n, MLA_PAD), _BF16)),
        compiler_params=pltpu.CompilerParams(
            dimension_semantics=("arbitrary",), vmem_limit_bytes=VMEM_LIMIT_BYTES),
        name="past_kv",
    )(lat2d, kr2d, w_kl, w_kr, w_v)


def _attn_kernel(q_ref, kp_ref, vp_ref, kd_ref, vd_ref, g_ref, o_ref, *, tq, tk, base_tiles,
                 tiles_per_step):
    j = pl.program_id(1)
    n_full = base_tiles + j * tiles_per_step
    rq = lax.broadcasted_iota(jnp.int32, (tq, tq), 0) // CHUNK
    ck = lax.broadcasted_iota(jnp.int32, (tq, tq), 1) // CHUNK
    visible = ck <= rq
    heads = []
    for hh in range(MLA_HEADS):
        sl = slice(hh * HEAD_PAD, (hh + 1) * HEAD_PAD)
        qh = q_ref[0, :, sl]
        s = jnp.where(visible, _dot_nt(qh, kd_ref[0, :, sl]), NEG_INF)
        m = jnp.max(s, axis=-1, keepdims=True)
        p = jnp.exp(s - m)
        l = jnp.sum(p, axis=-1, keepdims=True)
        acc = _dot(p.astype(_BF16), vd_ref[0, :, sl])

        def body(i, carry, qh=qh, sl=sl):
            m, l, acc = carry
            k0 = pl.multiple_of(i * tk, tk)
            s = _dot_nt(qh, kp_ref[0, pl.ds(k0, tk), sl])
            m_new = jnp.maximum(m, jnp.max(s, axis=-1, keepdims=True))
            alpha = jnp.exp(m - m_new)
            p = jnp.exp(s - m_new)
            l = alpha * l + jnp.sum(p, axis=-1, keepdims=True)
            acc = alpha * acc + _dot(p.astype(_BF16), vp_ref[0, pl.ds(k0, tk), sl])
            return m_new, l, acc

        m, l, acc = lax.fori_loop(0, n_full, body, (m, l, acc))
        heads.append(acc / l)
    pairs = [heads[2 * p] + heads[2 * p + 1] for p in range(MLA_HEADS // 2)]
    attn = jnp.concatenate(pairs, axis=1)
    o_ref[0] = _rms(attn, g_ref[...]).astype(_BF16)


def _attn_call(q, k_past, v_past, k_new, v_new, out_gain, offset):
    bsz, seq, _ = q.shape
    past_len = k_past.shape[1]
    tq = SEQ_TILE if seq % SEQ_TILE == 0 else seq
    n_q = seq // tq
    if offset == 0:
        tk = tq
        base_tiles, tiles_per_step = 0, 1
    else:
        assert n_q == 1
        tk = 512 if past_len % 512 == 0 else SEQ_TILE
        assert past_len == offset and offset % tk == 0
        base_tiles, tiles_per_step = offset // tk, 0
    return pl.pallas_call(
        functools.partial(_attn_kernel, tq=tq, tk=tk, base_tiles=base_tiles,
                          tiles_per_step=tiles_per_step),
        grid=(bsz, n_q),
        in_specs=[
            pl.BlockSpec((1, tq, MLA_PAD), lambda b, j: (b, j, 0)),
            pl.BlockSpec((1, past_len, MLA_PAD), lambda b, j: (b, 0, 0)),
            pl.BlockSpec((1, past_len, MLA_PAD), lambda b, j: (b, 0, 0)),
            pl.BlockSpec((1, tq, MLA_PAD), lambda b, j: (b, j, 0)),
            pl.BlockSpec((1, tq, MLA_PAD), lambda b, j: (b, j, 0)),
            _const_spec(out_gain.shape),
        ],
        out_specs=pl.BlockSpec((1, tq, MLA_WIDTH), lambda b, j: (b, j, 0)),
        out_shape=jax.ShapeDtypeStruct((bsz, seq, MLA_WIDTH), _BF16),
        compiler_params=pltpu.CompilerParams(
            dimension_semantics=("arbitrary", "arbitrary"), vmem_limit_bytes=VMEM_LIMIT_BYTES),
        name="mla_attn",
    )(q, k_past, v_past, k_new, v_new, out_gain)


def _rope_tables(pos):
    half = QK_ROPE // 2
    inv = ROPE_BASE ** (-jnp.arange(half, dtype=_F32) / half)
    ang = pos.astype(_F32)[:, None] * inv
    cos, sin = jnp.cos(ang), jnp.sin(ang)
    n = pos.shape[0]
    z = lambda w: jnp.zeros((n, w), _F32)
    rope_q = jnp.stack([
        jnp.concatenate([jnp.ones((n, QK_NOPE), _F32), cos, cos, z(HEAD_PAD - QK_NOPE - QK_ROPE)], 1),
        jnp.concatenate([z(QK_NOPE), -sin, z(HEAD_PAD - QK_NOPE - half)], 1),
        jnp.concatenate([z(QK_NOPE + half), sin, z(HEAD_PAD - QK_NOPE - QK_ROPE)], 1),
    ])
    rope_k = jnp.stack([
        jnp.concatenate([cos, cos, z(LANES - QK_ROPE)], 1),
        jnp.concatenate([-sin, z(LANES - half)], 1),
        jnp.concatenate([z(half), sin, z(LANES - QK_ROPE)], 1),
    ])
    return rope_q, rope_k


def _layer_params(l, norm_gains, ffa_w_gu, ffa_w_down, ffb_w_gu, ffb_w_down, w_in, q_norm_gain,
                  w_uq, kv_norm_gain, w_ukv, mla_out_gain, hg_norm_gain, w_out):
    half = QK_ROPE // 2
    o_kr = Q_LORA + KV_LORA
    wi = w_in[l]
    w_in_p = jnp.concatenate(
        [wi[:, :o_kr], wi[:, o_kr + QK_ROPE:], wi[:, o_kr:o_kr + QK_ROPE],
         jnp.zeros((D_MODEL, LANES - QK_ROPE), _F32)], axis=1).astype(_BF16)
    wq = w_uq[l].reshape(Q_LORA, MLA_HEADS, QK_NOPE + QK_ROPE)
    w_uq_p = jnp.concatenate(
        [wq, jnp.zeros((Q_LORA, MLA_HEADS, HEAD_PAD - QK_NOPE - QK_ROPE), _F32)], axis=2
    ).reshape(Q_LORA, MLA_PAD).astype(_BF16)
    wkv = w_ukv[l].reshape(KV_LORA, MLA_HEADS, QK_NOPE + V_DIM)
    w_kl = jnp.concatenate(
        [wkv[:, :, :QK_NOPE], jnp.zeros((KV_LORA, MLA_HEADS, HEAD_PAD - QK_NOPE), _F32)], axis=2
    ).reshape(KV_LORA, MLA_PAD).astype(_BF16)
    place = jnp.concatenate(
        [jnp.zeros((QK_ROPE, QK_NOPE), _F32), jnp.eye(QK_ROPE, dtype=_F32),
         jnp.zeros((QK_ROPE, HEAD_PAD - QK_NOPE - QK_ROPE), _F32)], axis=1)
    w_kr = jnp.concatenate(
        [jnp.tile(place, (1, MLA_HEADS)), jnp.zeros((LANES - QK_ROPE, MLA_PAD), _F32)], axis=0
    ).astype(_BF16)
    wv = wkv[:, :, QK_NOPE:]
    zv = jnp.zeros_like(wv)
    even = (jnp.arange(MLA_HEADS) % 2 == 0)[None, :, None]
    w_v = jnp.concatenate([jnp.where(even, wv, zv), jnp.where(even, zv, wv)], axis=2
                          ).reshape(KV_LORA, MLA_PAD).astype(_BF16)
    return dict(
        gains=norm_gains[l],
        ffa_gu=ffa_w_gu[l].astype(_BF16), ffa_down=ffa_w_down[l].astype(_BF16),
        ffb_gu=ffb_w_gu[l].astype(_BF16), ffb_down=ffb_w_down[l].astype(_BF16),
        w_in=w_in_p, q_gain=q_norm_gain[l][None, :], w_uq=w_uq_p,
        kv_gain=kv_norm_gain[l][None, :], w_kl=w_kl, w_kr=w_kr, w_v=w_v,
        out_gain=mla_out_gain[l][None, :], hg_gain=hg_norm_gain[l][None, :],
        w_out=w_out[l].astype(_BF16),
    )


def _layer(x, p, rope_q, rope_k, lb_logits, layer, lat_past, kr_past, s0):
    bsz, seq, _ = x.shape
    n = bsz * seq
    x2 = _ffn_call(x.reshape(n, D_MODEL), p['gains'], p['ffa_gu'], p['ffa_down'], (0, 1))
    q, k, v, lat, kr, ohg, state = _proj_call(
        x2.reshape(bsz, seq, D_MODEL), p['gains'], p['w_in'], p['q_gain'], p['w_uq'], p['kv_gain'],
        p['w_kl'], p['w_kr'], p['w_v'], rope_q, rope_k, lb_logits, p['hg_gain'], s0, layer)
    if lat_past is None:
        attn = _attn_call(q, k, v, k, v, p['out_gain'], 0)
    else:
        past = lat_past.shape[1]
        k_past, v_past = _past_kv_call(lat_past.reshape(bsz * past, KV_LORA),
                                       kr_past.reshape(bsz * past, QK_ROPE),
                                       p['w_kl'], p['w_kr'], p['w_v'])
        attn = _attn_call(q, k_past.reshape(bsz, past, MLA_PAD), v_past.reshape(bsz, past, MLA_PAD),
                          k, v, p['out_gain'], past)
    x3 = _out_call(x2, attn.reshape(n, MLA_WIDTH), ohg.reshape(n, HG_WIDTH), p['gains'],
                   p['w_out'], p['ffb_gu'], p['ffb_down'])
    return x3.reshape(bsz, seq, D_MODEL), lat, kr, state


def kernel(x_prompt, x_sample, cache_mla_latent, cache_mla_krope, state_hgrn, norm_gains, ffa_w_gu, ffa_w_down, ffb_w_gu, ffb_w_down, w_in, q_norm_gain, w_uq, kv_norm_gain, w_ukv, mla_out_gain, hg_lb_logits, hg_norm_gain, w_out):
    depth = norm_gains.shape[0]
    past_len = cache_mla_latent.shape[2]
    rope_p = _rope_tables(jnp.arange(x_prompt.shape[1]))
    rope_s = _rope_tables(past_len + jnp.arange(x_sample.shape[1]))
    lb_logits = hg_lb_logits.astype(_F32)
    h_p, h_s = x_prompt, x_sample
    outs = [[] for _ in range(6)]
    for l in range(depth):
        p = _layer_params(l, norm_gains, ffa_w_gu, ffa_w_down, ffb_w_gu, ffb_w_down, w_in,
                          q_norm_gain, w_uq, kv_norm_gain, w_ukv, mla_out_gain, hg_norm_gain, w_out)
        h_p, lat, kr, st = _layer(h_p, p, rope_p[0], rope_p[1], lb_logits, l, None, None, None)
        outs[0].append(lat)
        outs[1].append(kr)
        outs[2].append(st)
        h_s, lat, kr, st = _layer(h_s, p, rope_s[0], rope_s[1], lb_logits, l,
                                  cache_mla_latent[l], cache_mla_krope[l], state_hgrn[l])
        outs[3].append(lat)
        outs[4].append(kr)
        outs[5].append(st)
    return (h_p, h_s) + tuple(jnp.stack(o) for o in outs)
```

```python
import functools

import jax
import jax.numpy as jnp
from jax import lax
from jax.experimental import pallas as pl
from jax.experimental.pallas import tpu as pltpu

D_MODEL = 1024
CHUNK = 64
D_FF = 2816
FFN_RES_WEIGHT = 0.5
MLA_HEADS = 8
Q_LORA = 384
KV_LORA = 256
QK_NOPE = 64
QK_ROPE = 32
V_DIM = 64
MLA_WIDTH = MLA_HEADS * V_DIM
MLA_SCALE = (QK_NOPE + QK_ROPE) ** -0.5
LOG2_E = 1.4426950408889634
ROPE_BASE = 10000.0
HG_HEADS = 4
HG_DK = 128
HG_DV = 128
HG_KWIDTH = HG_HEADS * HG_DK
HG_WIDTH = HG_HEADS * HG_DV
NORM_EPS = 1e-6
NEG_INF = -1e30

LANES = 128
HEAD_PAD = LANES
MLA_PAD = MLA_HEADS * HEAD_PAD
VMEM_LIMIT_BYTES = 56 * 1024 * 1024

FF_CHUNK = 256
TOKEN_TILE = 512
SEQ_TILE = 256
IN_COLS_PAD = Q_LORA + KV_LORA + 2 * HG_KWIDTH + 2 * HG_WIDTH + LANES
HG_FAST_EXP_LIMIT = 75.0

_BF16 = jnp.bfloat16
_F32 = jnp.float32


def _rms(x, g):
    ms = jnp.mean(x * x, axis=-1, keepdims=True)
    return x * lax.rsqrt(ms + NORM_EPS) * g


def _dot(a, b):
    return jnp.dot(a, b, preferred_element_type=_F32)


def _dot_nt(a, b):
    return lax.dot_general(a, b, (((1,), (1,)), ((), ())), preferred_element_type=_F32)


def _dot_tn(a, b):
    return lax.dot_general(a, b, (((0,), (0,)), ((), ())), preferred_element_type=_F32)


def _const_spec(shape):
    nd = len(shape)
    return pl.BlockSpec(shape, lambda *_: (0,) * nd, pipeline_mode=pl.Buffered(1))


def _ffn_half_step(x, g_in, g_out, wgu_ref, wdown_ref):
    h = _rms(x, g_in).astype(_BF16)
    acc = jnp.zeros(x.shape, _F32)
    for c in range(D_FF // FF_CHUNK):
        lo = c * FF_CHUNK
        gate = _dot(h, wgu_ref[:, lo:lo + FF_CHUNK])
        up = _dot(h, wgu_ref[:, D_FF + lo:D_FF + lo + FF_CHUNK])
        act = (gate * jax.nn.sigmoid(gate) * up).astype(_BF16)
        acc = acc + _dot(act, wdown_ref[lo:lo + FF_CHUNK, :])
    return x + FFN_RES_WEIGHT * _rms(acc, g_out)


def _ffn_kernel(x_ref, g_ref, wgu_ref, wdown_ref, o_ref, *, gain_rows):
    gi, go = gain_rows
    o_ref[...] = _ffn_half_step(x_ref[...], g_ref[gi:gi + 1, :], g_ref[go:go + 1, :],
                                wgu_ref, wdown_ref)


def _ffn_call(x2d, gains, wgu, wdown, gain_rows):
    n = x2d.shape[0]
    tm = TOKEN_TILE if n % TOKEN_TILE == 0 else n
    return pl.pallas_call(
        functools.partial(_ffn_kernel, gain_rows=gain_rows),
        grid=(n // tm,),
        in_specs=[
            pl.BlockSpec((tm, D_MODEL), lambda i: (i, 0)),
            _const_spec(gains.shape),
            _const_spec(wgu.shape),
            _const_spec(wdown.shape),
        ],
        out_specs=pl.BlockSpec((tm, D_MODEL), lambda i: (i, 0)),
        out_shape=jax.ShapeDtypeStruct(x2d.shape, _F32),
        compiler_params=pltpu.CompilerParams(
            dimension_semantics=("arbitrary",), vmem_limit_bytes=VMEM_LIMIT_BYTES),
        name="ffn",
    )(x2d, gains, wgu, wdown)


def _out_kernel(x_ref, attn_ref, hg_ref, g_ref, wout_ref, wgu_ref, wdown_ref, o_ref):
    y = _dot(attn_ref[...], wout_ref[0:MLA_WIDTH, :]) + _dot(hg_ref[...], wout_ref[MLA_WIDTH:, :])
    x = x_ref[...] + _rms(y, g_ref[3:4, :])
    o_ref[...] = _ffn_half_step(x, g_ref[4:5, :], g_ref[5:6, :], wgu_ref, wdown_ref)


def _out_call(x2d, attn2d, hg2d, gains, wout, wgu, wdown):
    n = x2d.shape[0]
    tm = TOKEN_TILE if n % TOKEN_TILE == 0 else n
    return pl.pallas_call(
        _out_kernel,
        grid=(n // tm,),
        in_specs=[
            pl.BlockSpec((tm, D_MODEL), lambda i: (i, 0)),
            pl.BlockSpec((tm, MLA_WIDTH), lambda i: (i, 0)),
            pl.BlockSpec((tm, HG_WIDTH), lambda i: (i, 0)),
            _const_spec(gains.shape),
            _const_spec(wout.shape),
            _const_spec(wgu.shape),
            _const_spec(wdown.shape),
        ],
        out_specs=pl.BlockSpec((tm, D_MODEL), lambda i: (i, 0)),
        out_shape=jax.ShapeDtypeStruct(x2d.shape, _F32),
        compiler_params=pltpu.CompilerParams(
            dimension_semantics=("arbitrary",), vmem_limit_bytes=VMEM_LIMIT_BYTES),
        name="out_ffn",
    )(x2d, attn2d, hg2d, gains, wout, wgu, wdown)


def _chunk_cumsum(x, row_in_chunk):
    shift = 1
    while shift < CHUNK:
        moved = pltpu.roll(x, shift, 0)
        x = x + jnp.where(row_in_chunk >= shift, moved, 0.0)
        shift *= 2
    return x


def _hgrn_lower_bound(logits, layer):
    m = jnp.max(logits, axis=0, keepdims=True)
    e = jnp.exp(logits - m)
    den = jnp.sum(e, axis=0, keepdims=True)
    num = jnp.zeros_like(den)
    for j in range(1, layer + 1):
        num = num + e[j:j + 1, :]
    return num / den


def _proj_kernel(*refs, layer, tq, has_s0, v_transposed):
    if has_s0:
        (x_ref, g_ref, win_ref, qg_ref, wuq_ref, kvg_ref, wkl_ref, wkr_ref, wv_ref,
         ropeq_ref, ropek_ref, lbl_ref, hgg_ref, s0_ref,
         q_ref, k_ref, v_ref, lat_ref, kr_ref, ohg_ref, st_out_ref,
         st_ref, ks_ref, vs_ref, as_ref, oi_ref, oe_ref) = refs
    else:
        (x_ref, g_ref, win_ref, qg_ref, wuq_ref, kvg_ref, wkl_ref, wkr_ref, wv_ref,
         ropeq_ref, ropek_ref, lbl_ref, hgg_ref,
         q_ref, k_ref, v_ref, lat_ref, kr_ref, ohg_ref, st_out_ref,
         st_ref, ks_ref, vs_ref, as_ref, oi_ref, oe_ref) = refs
        s0_ref = None
    t = pl.program_id(1)
    n_t = pl.num_programs(1)
    n_chunks = tq // CHUNK

    @pl.when(t == 0)
    def _():
        for hh in range(HG_HEADS):
            if has_s0:
                st_ref[hh] = jnp.transpose(s0_ref[0, hh])
            else:
                st_ref[hh] = jnp.zeros((HG_DV, HG_DK), _F32)

    h = _rms(x_ref[0], g_ref[2:3, :]).astype(_BF16)
    proj = _dot(h, win_ref[...])
    o_q = 0
    o_kv = Q_LORA
    o_hq = o_kv + KV_LORA
    o_hf = o_hq + HG_KWIDTH
    o_hi = o_hf + HG_KWIDTH
    o_hgate = o_hi + HG_WIDTH
    o_kr = o_hgate + HG_WIDTH

    cqn = _rms(proj[:, o_q:o_q + Q_LORA], qg_ref[...]).astype(_BF16)
    qfull = _dot(cqn, wuq_ref[...]) * (MLA_SCALE * LOG2_E)
    cq_t, s1q, s2q = ropeq_ref[0], ropeq_ref[1], ropeq_ref[2]
    for hh in range(MLA_HEADS):
        qh = qfull[:, hh * HEAD_PAD:(hh + 1) * HEAD_PAD]
        qh = (qh * cq_t + pltpu.roll(qh, HEAD_PAD - QK_ROPE // 2, 1) * s1q
              + pltpu.roll(qh, QK_ROPE // 2, 1) * s2q)
        q_ref[0, :, hh * HEAD_PAD:(hh + 1) * HEAD_PAD] = qh.astype(_BF16)

    ckvn = _rms(proj[:, o_kv:o_kv + KV_LORA], kvg_ref[...])
    lat_ref[0] = ckvn
    kr = proj[:, o_kr:o_kr + LANES]
    kr = (kr * ropek_ref[0] + pltpu.roll(kr, LANES - QK_ROPE // 2, 1) * ropek_ref[1]
          + pltpu.roll(kr, QK_ROPE // 2, 1) * ropek_ref[2])
    kr_ref[0] = kr[:, 0:QK_ROPE]
    ckvn_b = ckvn.astype(_BF16)
    k_ref[0] = (_dot(ckvn_b, wkl_ref[...]) + _dot(kr.astype(_BF16), wkr_ref[...])).astype(_BF16)
    if v_transposed:
        v_ref[0] = _dot_nt(wv_ref[...], ckvn_b).astype(_BF16)
    else:
        v_ref[0] = _dot(ckvn_b, wv_ref[...]).astype(_BF16)

    lb = _hgrn_lower_bound(lbl_ref[...], layer)
    log_lb = jnp.log(lb)
    log_1mlb = jnp.log1p(-lb)
    q = proj[:, o_hq:o_hq + HG_KWIDTH]
    z = proj[:, o_hf:o_hf + HG_KWIDTH]
    v = proj[:, o_hi:o_hi + HG_WIDTH]
    ez = jnp.exp(-jnp.abs(z))
    rz = 1.0 / (1.0 + ez)
    b = log_1mlb + (jnp.minimum(z, 0.0) - jnp.log1p(ez))
    log_f = jnp.maximum(log_lb, b) + jnp.log1p(jnp.exp(-jnp.abs(log_lb - b)))
    kk = (1.0 - lb) * jnp.where(z >= 0.0, ez * rz, rz)
    row_in_chunk = lax.broadcasted_iota(jnp.int32, (tq, HG_KWIDTH), 0) & (CHUNK - 1)
    a = _chunk_cumsum(log_f, row_in_chunk)

    def chunk_row(row):
        return jnp.concatenate(
            [jnp.broadcast_to(a[c * CHUNK + row:c * CHUNK + row + 1, :], (CHUNK, HG_KWIDTH))
             for c in range(n_chunks)], axis=0)

    a_first, a_mid, a_end = chunk_row(0), chunk_row(CHUNK // 2 - 1), chunk_row(CHUNK - 1)
    use_exact = jnp.max(jnp.maximum(a_first - a_mid, a_mid - a_end)) > HG_FAST_EXP_LIMIT
    qa = (q * jnp.exp(a)).astype(_BF16)
    qg = (q * jnp.exp(a - a_mid)).astype(_BF16)
    kg = (kk * jnp.exp(a_mid - a)).astype(_BF16)
    kdec = (kk * jnp.exp(a_end - a)).astype(_BF16)
    dec_end = jnp.exp(a_end)
    v_b = v.astype(_BF16)
    row_t = lax.broadcasted_iota(jnp.int32, (tq, tq), 0)
    col_t = lax.broadcasted_iota(jnp.int32, (tq, tq), 1)
    same_chunk_causal = (col_t <= row_t) & (col_t >= (row_t & -CHUNK))
    col_chunk = lax.broadcasted_iota(jnp.int32, (HG_DV, tq), 1) // CHUNK
    for hh in range(HG_HEADS):
        sl = slice(hh * HG_DK, (hh + 1) * HG_DK)
        amat = jnp.where(same_chunk_causal, _dot_nt(qg[:, sl], kg[:, sl]), 0.0).astype(_BF16)
        oi_ref[:, sl] = _dot(amat, v_b[:, sl])
        v_t = jnp.transpose(v[:, sl]).astype(_BF16)
        v_blocks = jnp.concatenate(
            [jnp.where(col_chunk == c, v_t, jnp.zeros_like(v_t)) for c in range(n_chunks)], axis=0)
        u_all = _dot(v_blocks, kdec[:, sl])
        st = st_ref[hh]
        starts = []
        for c in range(n_chunks):
            starts.append(st)
            st = st * dec_end[c * CHUNK:c * CHUNK + 1, sl] + u_all[c * HG_DV:(c + 1) * HG_DV, :]
        st_ref[hh] = st
        oe_all = _dot_nt(qa[:, sl], jnp.concatenate(starts, axis=0).astype(_BF16))
        for c in range(n_chunks):
            oe_ref[c * CHUNK:(c + 1) * CHUNK, sl] = oe_all[c * CHUNK:(c + 1) * CHUNK,
                                                           c * HG_DV:(c + 1) * HG_DV]

    @pl.when(use_exact)
    def _():
        row_c = lax.broadcasted_iota(jnp.int32, (CHUNK, 1), 0)
        for hh in range(HG_HEADS):
            sl = slice(hh * HG_DK, (hh + 1) * HG_DK)
            ks_ref[hh] = kk[:, sl]
            vs_ref[hh] = v[:, sl]
            as_ref[hh] = a[:, sl]
            for c in range(n_chunks):
                r0 = c * CHUNK
                q_c = q[r0:r0 + CHUNK, sl]
                a_c = a[r0:r0 + CHUNK, sl]

                def body(s, o, r0=r0, hh=hh, q_c=q_c, a_c=a_c):
                    a_s = as_ref[hh, pl.ds(r0 + s, 1), :]
                    k_s = ks_ref[hh, pl.ds(r0 + s, 1), :]
                    v_s = vs_ref[hh, pl.ds(r0 + s, 1), :]
                    w = q_c * jnp.exp(jnp.minimum(a_c - a_s, 0.0)) * k_s
                    col = jnp.sum(w, axis=1, keepdims=True)
                    col = jnp.where(row_c >= s, col, 0.0)
                    return o + col * v_s

                oi_ref[r0:r0 + CHUNK, sl] = lax.fori_loop(
                    0, CHUNK, body, jnp.zeros((CHUNK, HG_DV), _F32))

    gate = proj[:, o_hgate:o_hgate + HG_WIDTH]
    silu_gate = gate * jax.nn.sigmoid(gate)
    for hh in range(HG_HEADS):
        sl = slice(hh * HG_DV, (hh + 1) * HG_DV)
        o = _rms(oi_ref[:, sl] + oe_ref[:, sl], hgg_ref[...]) * silu_gate[:, sl]
        ohg_ref[0, :, sl] = o.astype(_BF16)

    @pl.when(t == n_t - 1)
    def _():
        for hh in range(HG_HEADS):
            st_out_ref[0, hh] = jnp.transpose(st_ref[hh])


def _proj_call(x, gains, w_in, q_gain, w_uq, kv_gain, w_kl, w_kr, w_v, rope_q, rope_k,
               lb_logits, hg_gain, s0, layer, v_transposed):
    bsz, seq, _ = x.shape
    tq = SEQ_TILE if seq % SEQ_TILE == 0 else seq
    assert seq % tq == 0 and tq % CHUNK == 0
    has_s0 = s0 is not None
    if v_transposed:
        v_shape = jax.ShapeDtypeStruct((bsz, MLA_WIDTH, seq), _BF16)
        v_spec = pl.BlockSpec((1, MLA_WIDTH, tq), lambda b, t: (b, 0, t))
    else:
        v_shape = jax.ShapeDtypeStruct((bsz, seq, MLA_PAD), _BF16)
        v_spec = pl.BlockSpec((1, tq, MLA_PAD), lambda b, t: (b, t, 0))
    seq_block = lambda w: pl.BlockSpec((1, tq, w), lambda b, t: (b, t, 0))
    in_specs = [
        seq_block(D_MODEL),
        _const_spec(gains.shape),
        _const_spec(w_in.shape),
        _const_spec(q_gain.shape),
        _const_spec(w_uq.shape),
        _const_spec(kv_gain.shape),
        _const_spec(w_kl.shape),
        _const_spec(w_kr.shape),
        _const_spec(w_v.shape),
        pl.BlockSpec((3, tq, LANES), lambda b, t: (0, t, 0)),
        pl.BlockSpec((3, tq, LANES), lambda b, t: (0, t, 0)),
        _const_spec(lb_logits.shape),
        _const_spec(hg_gain.shape),
    ]
    args = [x, gains, w_in, q_gain, w_uq, kv_gain, w_kl, w_kr, w_v, rope_q, rope_k,
            lb_logits, hg_gain]
    state_spec = pl.BlockSpec((1, HG_HEADS, HG_DK, HG_DV), lambda b, t: (b, 0, 0, 0))
    if has_s0:
        in_specs.append(state_spec)
        args.append(s0)
    out_shape = (
        jax.ShapeDtypeStruct((bsz, seq, MLA_PAD), _BF16),
        jax.ShapeDtypeStruct((bsz, seq, MLA_PAD), _BF16),
        v_shape,
        jax.ShapeDtypeStruct((bsz, seq, KV_LORA), _F32),
        jax.ShapeDtypeStruct((bsz, seq, QK_ROPE), _F32),
        jax.ShapeDtypeStruct((bsz, seq, HG_WIDTH), _BF16),
        jax.ShapeDtypeStruct((bsz, HG_HEADS, HG_DK, HG_DV), _F32),
    )
    out_specs = (
        seq_block(MLA_PAD), seq_block(MLA_PAD), v_spec,
        seq_block(KV_LORA), seq_block(QK_ROPE), seq_block(HG_WIDTH), state_spec,
    )
    scratch = ([pltpu.VMEM((HG_HEADS, HG_DV, HG_DK), _F32)]
               + [pltpu.VMEM((HG_HEADS, tq, HG_DK), _F32) for _ in range(3)]
               + [pltpu.VMEM((tq, HG_WIDTH), _F32) for _ in range(2)])
    return pl.pallas_call(
        functools.partial(_proj_kernel, layer=layer, tq=tq, has_s0=has_s0,
                          v_transposed=v_transposed),
        grid=(bsz, seq // tq),
        in_specs=in_specs,
        out_specs=out_specs,
        out_shape=out_shape,
        scratch_shapes=scratch,
        compiler_params=pltpu.CompilerParams(
            dimension_semantics=("arbitrary", "arbitrary"), vmem_limit_bytes=VMEM_LIMIT_BYTES),
        name="proj_hgrn",
    )(*args)


def _past_kv_kernel(lat_ref, kr_ref, wkl_ref, wkr_ref, wv_ref, k_ref, v_ref):
    lat = lat_ref[...].astype(_BF16)
    k_ref[...] = (_dot(lat, wkl_ref[...])
                  + _dot(kr_ref[...].astype(_BF16), wkr_ref[0:QK_ROPE, :])).astype(_BF16)
    v_ref[...] = _dot(lat, wv_ref[...]).astype(_BF16)


def _past_kv_call(lat2d, kr2d, w_kl, w_kr, w_v):
    n = lat2d.shape[0]
    tm = TOKEN_TILE if n % TOKEN_TILE == 0 else n
    return pl.pallas_call(
        _past_kv_kernel,
        grid=(n // tm,),
        in_specs=[
            pl.BlockSpec((tm, KV_LORA), lambda i: (i, 0)),
            pl.BlockSpec((tm, QK_ROPE), lambda i: (i, 0)),
            _const_spec(w_kl.shape),
            _const_spec(w_kr.shape),
            _const_spec(w_v.shape),
        ],
        out_specs=(pl.BlockSpec((tm, MLA_PAD), lambda i: (i, 0)),
                   pl.BlockSpec((tm, MLA_PAD), lambda i: (i, 0))),
        out_shape=(jax.ShapeDtypeStruct((n, MLA_PAD), _BF16),
                   jax.ShapeDtypeStruct((n, MLA_PAD), _BF16)),
        compiler_params=pltpu.CompilerParams(
            dimension_semantics=("arbitrary",), vmem_limit_bytes=VMEM_LIMIT_BYTES),
        name="past_kv",
    )(lat2d, kr2d, w_kl, w_kr, w_v)


def _attn_kernel(q_ref, kp_ref, vp_ref, kd_ref, vd_ref, g_ref, o_ref, m_ref, l_ref, acc_ref, *,
                 tq, tk, base_tiles, tiles_per_step):
    j = pl.program_id(1)
    n_full = base_tiles + j * tiles_per_step
    rq = lax.broadcasted_iota(jnp.int32, (tq, tq), 0) // CHUNK
    ck = lax.broadcasted_iota(jnp.int32, (tq, tq), 1) // CHUNK
    visible = ck <= rq
    for hh in range(MLA_HEADS):
        sl = slice(hh * HEAD_PAD, (hh + 1) * HEAD_PAD)
        s = jnp.where(visible, _dot_nt(q_ref[0, :, sl], kd_ref[0, :, sl]), NEG_INF)
        m = jnp.max(s, axis=-1, keepdims=True)
        p = jnp.exp2(s - m)
        m_ref[hh] = m
        l_ref[hh] = jnp.sum(p, axis=-1, keepdims=True)
        acc_ref[hh] = _dot(p.astype(_BF16), vd_ref[0, :, sl])

    def body(i, carry):
        k0 = pl.multiple_of(i * tk, tk)
        for hh in range(MLA_HEADS):
            sl = slice(hh * HEAD_PAD, (hh + 1) * HEAD_PAD)
            s = _dot_nt(q_ref[0, :, sl], kp_ref[0, pl.ds(k0, tk), sl])
            m = m_ref[hh]
            m_new = jnp.maximum(m, jnp.max(s, axis=-1, keepdims=True))
            alpha = jnp.exp2(m - m_new)
            p = jnp.exp2(s - m_new)
            m_ref[hh] = m_new
            l_ref[hh] = alpha * l_ref[hh] + jnp.sum(p, axis=-1, keepdims=True)
            acc_ref[hh] = alpha * acc_ref[hh] + _dot(p.astype(_BF16),
                                                     vp_ref[0, pl.ds(k0, tk), sl])
        return carry

    lax.fori_loop(0, n_full, body, 0)
    pairs = []
    for p in range(MLA_HEADS // 2):
        pairs.append(acc_ref[2 * p] / l_ref[2 * p] + acc_ref[2 * p + 1] / l_ref[2 * p + 1])
    attn = jnp.concatenate(pairs, axis=1)
    o_ref[0] = _rms(attn, g_ref[...]).astype(_BF16)


def _attn_call(q, k_past, v_past, k_new, v_new, out_gain, offset):
    bsz, seq, _ = q.shape
    past_len = k_past.shape[1]
    tq = SEQ_TILE if seq % SEQ_TILE == 0 else seq
    n_q = seq // tq
    if offset == 0:
        tk = tq
        base_tiles, tiles_per_step = 0, 1
    else:
        assert n_q == 1
        tk = 512 if past_len % 512 == 0 else SEQ_TILE
        assert past_len == offset and offset % tk == 0
        base_tiles, tiles_per_step = offset // tk, 0
    return pl.pallas_call(
        functools.partial(_attn_kernel, tq=tq, tk=tk, base_tiles=base_tiles,
                          tiles_per_step=tiles_per_step),
        grid=(bsz, n_q),
        in_specs=[
            pl.BlockSpec((1, tq, MLA_PAD), lambda b, j: (b, j, 0)),
            pl.BlockSpec((1, past_len, MLA_PAD), lambda b, j: (b, 0, 0)),
            pl.BlockSpec((1, past_len, MLA_PAD), lambda b, j: (b, 0, 0)),
            pl.BlockSpec((1, tq, MLA_PAD), lambda b, j: (b, j, 0)),
            pl.BlockSpec((1, tq, MLA_PAD), lambda b, j: (b, j, 0)),
            _const_spec(out_gain.shape),
        ],
        out_specs=pl.BlockSpec((1, tq, MLA_WIDTH), lambda b, j: (b, j, 0)),
        out_shape=jax.ShapeDtypeStruct((bsz, seq, MLA_WIDTH), _BF16),
        scratch_shapes=[
            pltpu.VMEM((MLA_HEADS, tq, 1), _F32),
            pltpu.VMEM((MLA_HEADS, tq, 1), _F32),
            pltpu.VMEM((MLA_HEADS, tq, HEAD_PAD), _F32),
        ],
        compiler_params=pltpu.CompilerParams(
            dimension_semantics=("arbitrary", "arbitrary"), vmem_limit_bytes=VMEM_LIMIT_BYTES),
        name="mla_attn",
    )(q, k_past, v_past, k_new, v_new, out_gain)


def _attn_prompt_kernel(jq_ref, jk_ref, q_ref, k_ref, vt_ref, g_ref, o_ref, *state_refs, tile):
    m_refs = state_refs[0:MLA_HEADS]
    l_refs = state_refs[MLA_HEADS:2 * MLA_HEADS]
    acc_refs = state_refs[2 * MLA_HEADS:3 * MLA_HEADS]
    step = pl.program_id(1)
    jq = jq_ref[step]
    jk = jk_ref[step]

    @pl.when(jk == 0)
    def _():
        for hh in range(MLA_HEADS):
            m_refs[hh][...] = jnp.full((1, tile), NEG_INF, _F32)
            l_refs[hh][...] = jnp.zeros((1, tile), _F32)
            acc_refs[hh][...] = jnp.zeros((V_DIM, tile), _F32)

    def update(masked):
        if masked:
            key_chunk = lax.broadcasted_iota(jnp.int32, (tile, tile), 0) // CHUNK
            q_chunk = lax.broadcasted_iota(jnp.int32, (tile, tile), 1) // CHUNK
            visible = key_chunk <= q_chunk
        for hh in range(MLA_HEADS):
            sl = slice(hh * HEAD_PAD, (hh + 1) * HEAD_PAD)
            s = _dot_nt(k_ref[0, :, sl], q_ref[0, :, sl])
            if masked:
                s = jnp.where(visible, s, NEG_INF)
            m_old = m_refs[hh][...]
            m_new = jnp.maximum(m_old, jnp.max(s, axis=0, keepdims=True))
            alpha = jnp.exp2(m_old - m_new)
            p = jnp.exp2(s - m_new)
            m_refs[hh][...] = m_new
            l_refs[hh][...] = alpha * l_refs[hh][...] + jnp.sum(p, axis=0, keepdims=True)
            acc_refs[hh][...] = alpha * acc_refs[hh][...] + _dot(
                vt_ref[0, hh * V_DIM:(hh + 1) * V_DIM, :], p.astype(_BF16))

    @pl.when(jk < jq)
    def _():
        update(False)

    @pl.when(jk == jq)
    def _():
        update(True)
        heads = [acc_refs[hh][...] * (1.0 / l_refs[hh][...]) for hh in range(MLA_HEADS)]
        attn = jnp.transpose(jnp.concatenate(heads, axis=0))
        o_ref[0] = _rms(attn, g_ref[...]).astype(_BF16)


def _attn_prompt_call(q, k, v_t, out_gain):
    bsz, seq, _ = q.shape
    tile = 512 if seq % 512 == 0 else SEQ_TILE
    assert seq % tile == 0
    n_q = seq // tile
    pairs = [(j, t) for j in range(n_q) for t in range(j + 1)]
    jq = jnp.asarray([p[0] for p in pairs], jnp.int32)
    jk = jnp.asarray([p[1] for p in pairs], jnp.int32)
    grid_spec = pltpu.PrefetchScalarGridSpec(
        num_scalar_prefetch=2,
        grid=(bsz, len(pairs)),
        in_specs=[
            pl.BlockSpec((1, tile, MLA_PAD), lambda b, s, jq, jk: (b, jq[s], 0)),
            pl.BlockSpec((1, tile, MLA_PAD), lambda b, s, jq, jk: (b, jk[s], 0)),
            pl.BlockSpec((1, MLA_WIDTH, tile), lambda b, s, jq, jk: (b, 0, jk[s])),
            _const_spec(out_gain.shape),
        ],
        out_specs=pl.BlockSpec((1, tile, MLA_WIDTH), lambda b, s, jq, jk: (b, jq[s], 0)),
        scratch_shapes=([pltpu.VMEM((1, tile), _F32) for _ in range(2 * MLA_HEADS)]
                        + [pltpu.VMEM((V_DIM, tile), _F32) for _ in range(MLA_HEADS)]),
    )
    return pl.pallas_call(
        functools.partial(_attn_prompt_kernel, tile=tile),
        grid_spec=grid_spec,
        out_shape=jax.ShapeDtypeStruct((bsz, seq, MLA_WIDTH), _BF16),
        compiler_params=pltpu.CompilerParams(
            dimension_semantics=("arbitrary", "arbitrary"), vmem_limit_bytes=VMEM_LIMIT_BYTES),
        name="mla_attn_prompt",
    )(jq, jk, q, k, v_t, out_gain)


def _rope_tables(pos):
    half = QK_ROPE // 2
    inv = ROPE_BASE ** (-jnp.arange(half, dtype=_F32) / half)
    ang = pos.astype(_F32)[:, None] * inv
    cos, sin = jnp.cos(ang), jnp.sin(ang)
    n = pos.shape[0]
    z = lambda w: jnp.zeros((n, w), _F32)
    rope_q = jnp.stack([
        jnp.concatenate([jnp.ones((n, QK_NOPE), _F32), cos, cos, z(HEAD_PAD - QK_NOPE - QK_ROPE)], 1),
        jnp.concatenate([z(QK_NOPE), -sin, z(HEAD_PAD - QK_NOPE - half)], 1),
        jnp.concatenate([z(QK_NOPE + half), sin, z(HEAD_PAD - QK_NOPE - QK_ROPE)], 1),
    ])
    rope_k = jnp.stack([
        jnp.concatenate([cos, cos, z(LANES - QK_ROPE)], 1),
        jnp.concatenate([-sin, z(LANES - half)], 1),
        jnp.concatenate([z(half), sin, z(LANES - QK_ROPE)], 1),
    ])
    return rope_q, rope_k


def _layer_params(l, norm_gains, ffa_w_gu, ffa_w_down, ffb_w_gu, ffb_w_down, w_in, q_norm_gain,
                  w_uq, kv_norm_gain, w_ukv, mla_out_gain, hg_norm_gain, w_out):
    half = QK_ROPE // 2
    o_kr = Q_LORA + KV_LORA
    wi = w_in[l]
    w_in_p = jnp.concatenate(
        [wi[:, :o_kr], wi[:, o_kr + QK_ROPE:], wi[:, o_kr:o_kr + QK_ROPE],
         jnp.zeros((D_MODEL, LANES - QK_ROPE), _F32)], axis=1).astype(_BF16)
    wq = w_uq[l].reshape(Q_LORA, MLA_HEADS, QK_NOPE + QK_ROPE)
    w_uq_p = jnp.concatenate(
        [wq, jnp.zeros((Q_LORA, MLA_HEADS, HEAD_PAD - QK_NOPE - QK_ROPE), _F32)], axis=2
    ).reshape(Q_LORA, MLA_PAD).astype(_BF16)
    wkv = w_ukv[l].reshape(KV_LORA, MLA_HEADS, QK_NOPE + V_DIM)
    w_kl = jnp.concatenate(
        [wkv[:, :, :QK_NOPE], jnp.zeros((KV_LORA, MLA_HEADS, HEAD_PAD - QK_NOPE), _F32)], axis=2
    ).reshape(KV_LORA, MLA_PAD).astype(_BF16)
    place = jnp.concatenate(
        [jnp.zeros((QK_ROPE, QK_NOPE), _F32), jnp.eye(QK_ROPE, dtype=_F32),
         jnp.zeros((QK_ROPE, HEAD_PAD - QK_NOPE - QK_ROPE), _F32)], axis=1)
    w_kr = jnp.concatenate(
        [jnp.tile(place, (1, MLA_HEADS)), jnp.zeros((LANES - QK_ROPE, MLA_PAD), _F32)], axis=0
    ).astype(_BF16)
    wv = wkv[:, :, QK_NOPE:]
    zv = jnp.zeros_like(wv)
    even = (jnp.arange(MLA_HEADS) % 2 == 0)[None, :, None]
    w_v = jnp.concatenate([jnp.where(even, wv, zv), jnp.where(even, zv, wv)], axis=2
                          ).reshape(KV_LORA, MLA_PAD).astype(_BF16)
    return dict(
        gains=norm_gains[l],
        ffa_gu=ffa_w_gu[l].astype(_BF16), ffa_down=ffa_w_down[l].astype(_BF16),
        ffb_gu=ffb_w_gu[l].astype(_BF16), ffb_down=ffb_w_down[l].astype(_BF16),
        w_in=w_in_p, q_gain=q_norm_gain[l][None, :], w_uq=w_uq_p,
        kv_gain=kv_norm_gain[l][None, :], w_kl=w_kl, w_kr=w_kr, w_v=w_v,
        w_v_t=wv.reshape(KV_LORA, MLA_WIDTH).T.astype(_BF16),
        out_gain=mla_out_gain[l][None, :], hg_gain=hg_norm_gain[l][None, :],
        w_out=w_out[l].astype(_BF16),
    )


def _layer(x, p, rope_q, rope_k, lb_logits, layer, lat_past, kr_past, s0):
    bsz, seq, _ = x.shape
    n = bsz * seq
    x2 = _ffn_call(x.reshape(n, D_MODEL), p['gains'], p['ffa_gu'], p['ffa_down'], (0, 1))
    is_prompt = lat_past is None
    q, k, v, lat, kr, ohg, state = _proj_call(
        x2.reshape(bsz, seq, D_MODEL), p['gains'], p['w_in'], p['q_gain'], p['w_uq'], p['kv_gain'],
        p['w_kl'], p['w_kr'], p['w_v_t'] if is_prompt else p['w_v'], rope_q, rope_k, lb_logits,
        p['hg_gain'], s0, layer, is_prompt)
    if is_prompt:
        attn = _attn_prompt_call(q, k, v, p['out_gain'])
    else:
        past = lat_past.shape[1]
        k_past, v_past = _past_kv_call(lat_past.reshape(bsz * past, KV_LORA),
                                       kr_past.reshape(bsz * past, QK_ROPE),
                                       p['w_kl'], p['w_kr'], p['w_v'])
        attn = _attn_call(q, k_past.reshape(bsz, past, MLA_PAD), v_past.reshape(bsz, past, MLA_PAD),
                          k, v, p['out_gain'], past)
    x3 = _out_call(x2, attn.reshape(n, MLA_WIDTH), ohg.reshape(n, HG_WIDTH), p['gains'],
                   p['w_out'], p['ffb_gu'], p['ffb_down'])
    return x3.reshape(bsz, seq, D_MODEL), lat, kr, state


def kernel(x_prompt, x_sample, cache_mla_latent, cache_mla_krope, state_hgrn, norm_gains, ffa_w_gu, ffa_w_down, ffb_w_gu, ffb_w_down, w_in, q_norm_gain, w_uq, kv_norm_gain, w_ukv, mla_out_gain, hg_lb_logits, hg_norm_gain, w_out):
    depth = norm_gains.shape[0]
    past_len = cache_mla_latent.shape[2]
    rope_p = _rope_tables(jnp.arange(x_prompt.shape[1]))
    rope_s = _rope_tables(past_len + jnp.arange(x_sample.shape[1]))
    lb_logits = hg_lb_logits.astype(_F32)
    h_p, h_s = x_prompt, x_sample
    outs = [[] for _ in range(6)]
    for l in range(depth):
        p = _layer_params(l, norm_gains, ffa_w_gu, ffa_w_down, ffb_w_gu, ffb_w_down, w_in,
                          q_norm_gain, w_uq, kv_norm_gain, w_ukv, mla_out_gain, hg_norm_gain, w_out)
        h_p, lat, kr, st = _layer(h_p, p, rope_p[0], rope_p[1], lb_logits, l, None, None, None)
        outs[0].append(lat)
        outs[1].append(kr)
        outs[2].append(st)
        h_s, lat, kr, st = _layer(h_s, p, rope_s[0], rope_s[1], lb_logits, l,
                                  cache_mla_latent[l], cache_mla_krope[l], state_hgrn[l])
        outs[3].append(lat)
        outs[4].append(kr)
        outs[5].append(st)
    return (h_p, h_s) + tuple(jnp.stack(o) for o in outs)
```
